```python
import math
import jax, jax.numpy as jnp
from jax import lax
import numpy as np


D_MODEL = 2048
BATCH = 16
SEQ = 2048
DEPTH = 2

HEAD_DIM = 128
A_HEADS = 4
IDX_HEADS = 16
IDX_DIM = 64
TOPK_MAX = 256
Q_BLOCK = 128
B_HEADS = 8
B_KV_HEADS = 2
WINDOW = 128
C_HEADS = 4
C_DK = 64
C_DV = 128
GATE_RANK = 16
GATE_NORM = 16.0
CHUNK = 64
N_BUCKETS = 32
MAX_DIST = 128
N_BIAS_HEADS = A_HEADS + B_HEADS
N_GROUPS = 8
EXP_PER_GROUP = 8
N_EXPERTS = N_GROUPS * EXP_PER_GROUP
TOP_K_EXP = 2
D_EXPERT = 512
MOE_BLOCK = 128
EPS = 1e-6

IN_SPLITS = (
    ('a_q', A_HEADS * HEAD_DIM), ('a_k', HEAD_DIM), ('a_v', HEAD_DIM),
    ('i_q', IDX_HEADS * IDX_DIM), ('i_k', IDX_DIM), ('i_w', IDX_HEADS),
    ('b_q', B_HEADS * HEAD_DIM), ('b_k', B_KV_HEADS * HEAD_DIM), ('b_v', B_KV_HEADS * HEAD_DIM),
    ('c_q', C_HEADS * C_DK), ('c_k', C_HEADS * C_DK), ('c_v', C_HEADS * C_DV),
    ('c_g', GATE_RANK), ('c_r', C_HEADS * C_DV),
    ('g_a', D_MODEL), ('g_b', D_MODEL), ('g_c', D_MODEL),
)
D_IN = sum(n for _, n in IN_SPLITS)

kernel_name = 'hybrid_dsa_swa_gla_hmoe'


def split_in(z):
    out = {}
    off = 0
    for name, n in IN_SPLITS:
        out[name] = z[..., off:off + n]
        off += n
    return out


def rms_norm(x, g):
    xf = x.astype(jnp.float32)
    y = xf * lax.rsqrt(jnp.mean(xf * xf, axis=-1, keepdims=True) + EPS)
    return (y * g.astype(jnp.float32)).astype(x.dtype)


def rel_bucket(dist):
    max_exact = N_BUCKETS // 2
    d = jnp.maximum(dist, 0)
    large = max_exact + (jnp.log(jnp.maximum(d, 1).astype(jnp.float32) / max_exact)
                         / math.log(MAX_DIST / max_exact) * (N_BUCKETS - max_exact)).astype(jnp.int32)
    large = jnp.minimum(large, N_BUCKETS - 1)
    return jnp.where(d < max_exact, d, large)


def dsa_attention(q, k, v, iq, ik, iw, bias_tab, n_keep):
    B, S, H, dh = q.shape
    nb = S // Q_BLOCK
    scale = dh ** -0.5
    iscale = (IDX_HEADS ** -0.5) * (IDX_DIM ** -0.5)
    s_pos = jnp.arange(S)
    take = jax.vmap(lambda a, i: a[i])

    def block(i):
        t0 = i * Q_BLOCK
        t = t0 + jnp.arange(Q_BLOCK)
        q_b = lax.dynamic_slice_in_dim(q, t0, Q_BLOCK, axis=1)
        iq_b = lax.dynamic_slice_in_dim(iq, t0, Q_BLOCK, axis=1)
        iw_b = lax.dynamic_slice_in_dim(iw, t0, Q_BLOCK, axis=1)
        rel = jax.nn.relu(jnp.einsum('bqhd,bsd->bqsh', iq_b, ik).astype(jnp.float32))
        score = jnp.einsum('bqsh,bqh->bqs', rel, iw_b.astype(jnp.float32)) * iscale
        score = jnp.where(s_pos[None, None, :] <= t[None, :, None], score, -jnp.inf)
        _, sel = lax.top_k(score, n_keep)
        valid = sel <= t[None, :, None]
        k_sel = take(k, sel)
        v_sel = take(v, sel)
        logits = jnp.einsum('bqhd,bqkd->bhqk', q_b, k_sel).astype(jnp.float32) * scale
        bias = bias_tab[rel_bucket(t[None, :, None] - sel)]
        logits = logits + bias.transpose(0, 3, 1, 2)
        logits = jnp.where(valid[:, None], logits, -jnp.inf)
        w = jax.nn.softmax(logits, axis=-1)
        return jnp.einsum('bhqk,bqkd->bqhd', w.astype(v.dtype), v_sel)

    out = lax.map(block, jnp.arange(nb))
    return out.transpose(1, 0, 2, 3, 4).reshape(B, S, H * dh).astype(q.dtype)


def sliding_window_attention(q, k, v, sinks, bias_tab):
    B, S, Hq, dh = q.shape
    Hkv = k.shape[2]
    G = Hq // Hkv
    nb = S // WINDOW
    scale = dh ** -0.5
    qb = q.reshape(B, nb, WINDOW, Hkv, G, dh)
    kb = k.reshape(B, nb, WINDOW, Hkv, dh)
    vb = v.reshape(B, nb, WINDOW, Hkv, dh)
    pad = ((0, 0), (1, 0), (0, 0), (0, 0), (0, 0))
    kw = jnp.concatenate([jnp.pad(kb, pad)[:, :-1], kb], axis=2)
    vw = jnp.concatenate([jnp.pad(vb, pad)[:, :-1], vb], axis=2)
    logits = jnp.einsum('bnqhgd,bnshd->bnhgqs', qb, kw).astype(jnp.float32) * scale
    i = jnp.arange(WINDOW)[:, None]
    j = jnp.arange(2 * WINDOW)[None, :]
    dist = i + WINDOW - j
    band = (dist >= 0) & (dist < WINDOW)
    key_ok = (jnp.arange(nb)[:, None, None] * WINDOW + j[None] - WINDOW) >= 0
    mask = band[None] & key_ok
    bias = bias_tab[rel_bucket(dist)].reshape(WINDOW, 2 * WINDOW, Hkv, G).transpose(2, 3, 0, 1)
    logits = jnp.where(mask[None, :, None, None], logits + bias, -jnp.inf)
    sink = sinks.astype(jnp.float32).reshape(Hkv, G)[None, None, :, :, None, None]
    m = jnp.maximum(jnp.max(logits, axis=-1, keepdims=True), sink)
    e = jnp.exp(logits - m)
    p = e / (jnp.sum(e, axis=-1, keepdims=True) + jnp.exp(sink - m))
    out = jnp.einsum('bnhgqs,bnshd->bnqhgd', p.astype(v.dtype), vw)
    return out.reshape(B, S, Hq * dh).astype(q.dtype)


def gla(q, k, v, log_a):
    B, S, H, dk = q.shape
    dv = v.shape[-1]
    nc = S // CHUNK
    f = lambda z: z.astype(jnp.float32).reshape(B, nc, CHUNK, H, -1).transpose(1, 0, 3, 2, 4)
    q, k, v, la = f(q) * dk ** -0.5, f(k), f(v), f(log_a)
    b = jnp.cumsum(la, axis=3)
    b_last = b[..., -1:, :]
    qg = q * jnp.exp(b)
    kg = k * jnp.exp(-b)
    kdec = k * jnp.exp(b_last - b)
    causal = jnp.tril(jnp.ones((CHUNK, CHUNK), dtype=bool))
    att = jnp.where(causal, jnp.einsum('nbhcd,nbhsd->nbhcs', qg, kg), 0.0)
    o_intra = jnp.einsum('nbhcs,nbhsv->nbhcv', att, v)
    chunk_state = jnp.einsum('nbhsd,nbhsv->nbhdv', kdec, v)
    decay = jnp.exp(b_last[..., 0, :])

    def step(state, inp):
        dec, cs = inp
        return dec[..., None] * state + cs, state

    _, prev_states = lax.scan(step, jnp.zeros((B, H, dk, dv), jnp.float32), (decay, chunk_state))
    o_inter = jnp.einsum('nbhcd,nbhdv->nbhcv', qg, prev_states)
    return (o_intra + o_inter).transpose(1, 0, 3, 2, 4).reshape(B, S, H, dv)


def hier_moe(xn, w_rg, b_rg, w_re, b_re, w1, w3, w2):
    B, S, D = xn.shape
    T = B * S
    A = T * TOP_K_EXP
    xt = xn.reshape(T, D)
    lg = (xt @ w_rg).astype(jnp.float32) + b_rg.astype(jnp.float32)
    pg = jax.nn.softmax(lg, axis=-1)
    grp = jnp.argmax(lg, axis=-1)
    gw = jnp.take_along_axis(pg, grp[:, None], axis=1)[:, 0]
    le = ((xt @ w_re).astype(jnp.float32) + b_re.astype(jnp.float32)).reshape(T, N_GROUPS, EXP_PER_GROUP)
    le = jnp.take_along_axis(le, grp[:, None, None], axis=1)[:, 0]
    top_p, top_i = lax.top_k(jax.nn.softmax(le, axis=-1), TOP_K_EXP)
    top_p = top_p / jnp.sum(top_p, axis=-1, keepdims=True)
    eid = (grp[:, None] * EXP_PER_GROUP + top_i).reshape(-1)
    wts = (gw[:, None] * top_p).reshape(-1)
    tok = jnp.arange(A) // TOP_K_EXP
    order = jnp.argsort(eid)
    se = eid[order]
    counts = jnp.bincount(eid, length=N_EXPERTS)
    starts = jnp.cumsum(counts) - counts
    pcounts = (counts + MOE_BLOCK - 1) // MOE_BLOCK * MOE_BLOCK
    pends = jnp.cumsum(pcounts)
    pstarts = pends - pcounts
    dest = pstarts[se] + (jnp.arange(A) - starts[se])
    P = A + N_EXPERTS * MOE_BLOCK
    nblk = P // MOE_BLOCK
    row_tok = jnp.full((P,), T, jnp.int32).at[dest].set(tok[order].astype(jnp.int32))
    row_w = jnp.zeros((P,), jnp.float32).at[dest].set(wts[order])
    blk_exp = jnp.minimum(jnp.searchsorted(pends, jnp.arange(nblk) * MOE_BLOCK, side='right'), N_EXPERTS - 1)
    x_pad = jnp.concatenate([xt, jnp.zeros((1, D), xt.dtype)], axis=0)

    def expert_block(args):
        rows, e = args
        xb = x_pad[rows]
        hdn = jax.nn.silu(xb @ w1[e]) * (xb @ w3[e])
        return hdn @ w2[e]

    ys = lax.map(expert_block, (row_tok.reshape(nblk, MOE_BLOCK), blk_exp)).reshape(P, D)
    y = jax.ops.segment_sum(ys.astype(jnp.float32) * row_w[:, None], row_tok, num_segments=T + 1)[:T]
    return y.reshape(B, S, D).astype(xn.dtype)


def setup_inputs(seed: int = 0) -> dict:
    key = jax.random.key(seed)
    ks = jax.random.split(key, 24)
    L, D = DEPTH, D_MODEL
    nrm = lambda k, s: jax.random.normal(k, s, jnp.float32)
    w = lambda k, s, fan_in: nrm(k, s) * fan_in ** -0.5
    gain = lambda k, s: 1.0 + 0.02 * nrm(k, s)
    return {
        'x': nrm(ks[0], (BATCH, SEQ, D)),
        'rel_bias': 0.1 * nrm(ks[1], (N_BUCKETS, N_BIAS_HEADS)),
        'norm1_g': gain(ks[2], (L, D)),
        'w_in': w(ks[3], (L, D, D_IN), D),
        'a_qn_g': gain(ks[4], (L, HEAD_DIM)),
        'a_kn_g': gain(ks[5], (L, HEAD_DIM)),
        'b_qn_g': gain(ks[6], (L, HEAD_DIM)),
        'b_kn_g': gain(ks[7], (L, HEAD_DIM)),
        'b_sinks': nrm(ks[8], (L, B_HEADS)),
        'c_gate_w': w(ks[9], (L, GATE_RANK, C_HEADS * C_DK), GATE_RANK),
        'c_gate_b': 0.1 * nrm(ks[10], (L, C_HEADS * C_DK)),
        'c_norm_g': gain(ks[11], (L, C_DV)),
        'w_br_a': w(ks[12], (L, A_HEADS * HEAD_DIM, D), A_HEADS * HEAD_DIM),
        'w_br_b': w(ks[13], (L, B_HEADS * HEAD_DIM, D), B_HEADS * HEAD_DIM),
        'w_br_c': w(ks[14], (L, C_HEADS * C_DV, D), C_HEADS * C_DV),
        'w_o': w(ks[15], (L, D, D), D),
        'norm2_g': gain(ks[16], (L, D)),
        'w_rg': w(ks[17], (L, D, N_GROUPS), D),
        'b_rg': 0.01 * nrm(ks[18], (L, N_GROUPS)),
        'w_re': w(ks[19], (L, D, N_EXPERTS), D),
        'b_re': 0.01 * nrm(ks[20], (L, N_EXPERTS)),
        'w1': w(ks[21], (L, N_EXPERTS, D, D_EXPERT), D),
        'w3': w(ks[22], (L, N_EXPERTS, D, D_EXPERT), D),
        'w2': w(ks[23], (L, N_EXPERTS, D_EXPERT, D), D_EXPERT),
    }


def reference(x, rel_bias, norm1_g, w_in, a_qn_g, a_kn_g, b_qn_g, b_kn_g, b_sinks,
              c_gate_w, c_gate_b, c_norm_g, w_br_a, w_br_b, w_br_c, w_o, norm2_g,
              w_rg, b_rg, w_re, b_re, w1, w3, w2):
    B, S, _ = x.shape
    n_keep = min(TOPK_MAX, S // 4)
    bias_a = rel_bias[:, :A_HEADS]
    bias_b = rel_bias[:, A_HEADS:]
    h = x
    for l in range(DEPTH):
        xn = rms_norm(h, norm1_g[l])
        p = split_in(xn @ w_in[l])
        qa = rms_norm(p['a_q'].reshape(B, S, A_HEADS, HEAD_DIM), a_qn_g[l])
        ka = rms_norm(p['a_k'], a_kn_g[l])
        oa = dsa_attention(qa, ka, p['a_v'], p['i_q'].reshape(B, S, IDX_HEADS, IDX_DIM),
                           p['i_k'], p['i_w'], bias_a, n_keep)
        qb = rms_norm(p['b_q'].reshape(B, S, B_HEADS, HEAD_DIM), b_qn_g[l])
        kb = rms_norm(p['b_k'].reshape(B, S, B_KV_HEADS, HEAD_DIM), b_kn_g[l])
        ob = sliding_window_attention(qb, kb, p['b_v'].reshape(B, S, B_KV_HEADS, HEAD_DIM),
                                      b_sinks[l], bias_b)
        log_a = jax.nn.log_sigmoid((p['c_g'] @ c_gate_w[l] + c_gate_b[l]).astype(jnp.float32)) / GATE_NORM
        oc = gla(p['c_q'].reshape(B, S, C_HEADS, C_DK), p['c_k'].reshape(B, S, C_HEADS, C_DK),
                 p['c_v'].reshape(B, S, C_HEADS, C_DV), log_a.reshape(B, S, C_HEADS, C_DK))
        oc = rms_norm(oc, c_norm_g[l]) * jax.nn.silu(p['c_r'].reshape(B, S, C_HEADS, C_DV).astype(jnp.float32))
        oc = oc.reshape(B, S, C_HEADS * C_DV).astype(h.dtype)
        merged = (jax.nn.sigmoid(p['g_a']) * (oa @ w_br_a[l])
                  + jax.nn.sigmoid(p['g_b']) * (ob @ w_br_b[l])
                  + jax.nn.sigmoid(p['g_c']) * (oc @ w_br_c[l]))
        h = h + (merged @ w_o[l]).astype(h.dtype)
        h = h + hier_moe(rms_norm(h, norm2_g[l]), w_rg[l], b_rg[l], w_re[l], b_re[l], w1[l], w3[l], w2[l])
    return h
```

```python
import functools
import math

import jax
import jax.numpy as jnp
from jax import lax
from jax.experimental import pallas as pl
from jax.experimental.pallas import tpu as pltpu

F32 = jnp.float32
BF16 = jnp.bfloat16
I32 = jnp.int32

HEAD_DIM = 128
A_HEADS = 4
IDX_HEADS = 16
IDX_DIM = 64
TOPK_MAX = 256
B_HEADS = 8
B_KV_HEADS = 2
WINDOW = 128
C_HEADS = 4
C_DK = 64
C_DV = 128
GATE_RANK = 16
GATE_NORM = 16.0
CHUNK = 64
N_BUCKETS = 32
MAX_DIST = 128
N_GROUPS = 8
EXP_PER_GROUP = 8
N_EXPERTS = N_GROUPS * EXP_PER_GROUP
D_EXPERT = 512
EPS = 1e-6

LANES = 128
QB = 256
MOE_ROWS = 256
VMEM_LIMIT = 52 * 1024 * 1024

INT_MIN = -(2 ** 31)
NT = (((1,), (1,)), ((), ()))
TN = (((0,), (0,)), ((), ()))


def _cparams(*sem):
    return pltpu.CompilerParams(dimension_semantics=sem, vmem_limit_bytes=VMEM_LIMIT)


def _rms(x, g):
    ms = jnp.mean(x * x, axis=-1, keepdims=True)
    return x * lax.rsqrt(ms + EPS) * g


def _sigmoid(x):
    return 1.0 / (1.0 + jnp.exp(-x))


def _z_layout(d):
    order = (('g_a', d), ('g_b', d), ('g_c', d), ('i_q', 1024), ('b_q', 1024), ('a_q', 512), ('c_v', 512),
             ('c_r', 512), ('b_k', 256), ('b_v', 256), ('c_q', 256), ('c_k', 256), ('a_k', 128), ('a_v', 128),
             ('i_kw', 128), ('c_g', 128))
    off, o = {}, 0
    for name, n in order:
        assert o % n == 0
        off[name] = o
        o += n
    return off, o


def _pack_w_in(w, d):
    splits = (('a_q', 512), ('a_k', 128), ('a_v', 128), ('i_q', 1024), ('i_k', 64), ('i_w', 16), ('b_q', 1024),
              ('b_k', 256), ('b_v', 256), ('c_q', 256), ('c_k', 256), ('c_v', 512), ('c_g', 16), ('c_r', 512),
              ('g_a', d), ('g_b', d), ('g_c', d))
    src, o = {}, 0
    for name, n in splits:
        src[name] = w[:, o:o + n]
        o += n
    zpad = lambda n: jnp.zeros((w.shape[0], n), w.dtype)
    src['i_kw'] = jnp.concatenate([src['i_k'], src['i_w'], zpad(LANES - IDX_DIM - IDX_HEADS)], axis=1)
    src['c_g'] = jnp.concatenate([src['c_g'], zpad(LANES - GATE_RANK)], axis=1)
    off, nz = _z_layout(d)
    names = sorted(off, key=off.get)
    return jnp.concatenate([src[n] for n in names], axis=1).astype(BF16)


def _rel_bucket(dist):
    max_exact = N_BUCKETS // 2
    d = jnp.maximum(dist, 0)
    large = max_exact + (jnp.log(jnp.maximum(d, 1).astype(F32) / max_exact)
                         / math.log(MAX_DIST / max_exact) * (N_BUCKETS - max_exact)).astype(I32)
    large = jnp.minimum(large, N_BUCKETS - 1)
    return jnp.where(d < max_exact, d, large)


def _inproj_kernel(x_ref, g_ref, w_ref, o_ref, xn_ref):
    @pl.when(pl.program_id(1) == 0)
    def _():
        xn_ref[...] = _rms(x_ref[...], g_ref[...]).astype(BF16)

    o_ref[...] = jnp.dot(xn_ref[...], w_ref[...], preferred_element_type=F32)


def _inproj(h2d, g, wz):
    t, d = h2d.shape
    nz = wz.shape[1]
    tm, tn = 512, 1024
    return pl.pallas_call(
        _inproj_kernel, grid=(t // tm, nz // tn),
        in_specs=[pl.BlockSpec((tm, d), lambda i, j: (i, 0)),
                  pl.BlockSpec((1, d), lambda i, j: (0, 0)),
                  pl.BlockSpec((d, tn), lambda i, j: (0, j))],
        out_specs=pl.BlockSpec((tm, tn), lambda i, j: (i, j)),
        out_shape=jax.ShapeDtypeStruct((t, nz), F32),
        scratch_shapes=[pltpu.VMEM((tm, d), BF16)],
        compiler_params=_cparams("arbitrary", "arbitrary"),
        name="inproj")(h2d, g, wz)


def _dsa_kernel(aq_ref, iq_ref, ak_ref, av_ref, ikw_ref, qg_ref, kg_ref, bias_ref, o_ref,
                kn_ref, vt_ref, ik_ref, key_ref, lg_ref, *, n_keep):
    i = pl.program_id(1)
    s_len = ak_ref.shape[0]
    nch = s_len // QB
    row0 = pl.multiple_of(i * QB, QB)

    @pl.when(i == 0)
    def _prep():
        kn_ref[...] = _rms(ak_ref[...], kg_ref[...]).astype(BF16)
        ik_ref[...] = ikw_ref[:, :IDX_DIM].astype(BF16)
        for c in range(nch):
            vt_ref[c] = av_ref[c * QB:(c + 1) * QB, :].T.astype(BF16)

    key_pos = lax.broadcasted_iota(I32, (QB, QB), 0)
    qry_pos = lax.broadcasted_iota(I32, (QB, QB), 1) + row0

    iw_t = ikw_ref[pl.ds(row0, QB), :].T
    iq = iq_ref[...].astype(BF16)
    iscale = (IDX_HEADS ** -0.5) * (IDX_DIM ** -0.5)

    def idx_body(c, carry):
        rows = pl.ds(pl.multiple_of(c * QB, QB), QB)
        ikc = ik_ref[rows, :]
        acc = jnp.zeros((QB, QB), F32)
        for h in range(IDX_HEADS):
            rel = lax.dot_general(ikc, iq[:, h * IDX_DIM:(h + 1) * IDX_DIM], NT, preferred_element_type=F32)
            acc = acc + jnp.maximum(rel, 0.0) * iw_t[IDX_DIM + h:IDX_DIM + h + 1, :]
        score = jnp.where(key_pos + c * QB <= qry_pos, acc * iscale, -jnp.inf)
        bits = lax.bitcast_convert_type(score, I32)
        key_ref[rows, :] = jnp.where(bits < 0, bits ^ jnp.int32(0x7FFFFFFF), bits)
        return carry

    lax.fori_loop(0, i + 1, idx_body, 0)

    def bit_body(it, thr):
        cand = thr + lax.shift_left(jnp.int32(1), 31 - it)

        def cnt_body(c, cnt):
            key = key_ref[pl.ds(pl.multiple_of(c * QB, QB), QB), :]
            ge = jnp.where(key >= cand, 1.0, 0.0)
            return cnt + jnp.sum(ge.reshape(QB // 8, 8, QB), axis=0)

        cnt = lax.fori_loop(0, i + 1, cnt_body, jnp.zeros((8, QB), F32))
        tot = jnp.sum(cnt, axis=0, keepdims=True)
        return jnp.where(tot >= n_keep, cand, thr)

    thr = lax.fori_loop(0, 32, bit_body, jnp.full((1, QB), INT_MIN, I32))

    scale = HEAD_DIM ** -0.5
    aq = aq_ref[...]
    for h in range(A_HEADS):
        qn = _rms(aq[:, h * HEAD_DIM:(h + 1) * HEAD_DIM], qg_ref[...]).astype(BF16)

        def logit_body(c, m, qn=qn, h=h):
            rows = pl.ds(pl.multiple_of(c * QB, QB), QB)
            lt = lax.dot_general(kn_ref[rows, :], qn, NT, preferred_element_type=F32) * scale
            lt = lt + bias_ref[h, jnp.minimum(i - c, 2)]
            keep = (key_ref[rows, :] >= thr) & (key_pos + c * QB <= qry_pos)
            lt = jnp.where(keep, lt, -jnp.inf)
            lg_ref[rows, :] = lt
            return jnp.maximum(m, jnp.max(lt, axis=0, keepdims=True))

        m = lax.fori_loop(0, i + 1, logit_body, jnp.full((1, QB), -jnp.inf, F32))

        def pv_body(c, carry, m=m):
            l, acc = carry
            e = jnp.exp(lg_ref[pl.ds(pl.multiple_of(c * QB, QB), QB), :] - m)
            l = l + jnp.sum(e, axis=0, keepdims=True)
            acc = acc + jnp.dot(vt_ref[c], e.astype(BF16), preferred_element_type=F32)
            return l, acc

        l, acc = lax.fori_loop(0, i + 1, pv_body,
                               (jnp.zeros((1, QB), F32), jnp.zeros((HEAD_DIM, QB), F32)))
        o_ref[:, h * HEAD_DIM:(h + 1) * HEAD_DIM] = (acc / l).T.astype(o_ref.dtype)


def _dsa(z, off, qg, kg, bias_t, b, s, n_keep):
    nq = s // QB
    col = lambda name, w: off[name] // w
    return pl.pallas_call(
        functools.partial(_dsa_kernel, n_keep=n_keep), grid=(b, nq),
        in_specs=[pl.BlockSpec((QB, 512), lambda bi, i: (bi * nq + i, col('a_q', 512))),
                  pl.BlockSpec((QB, 1024), lambda bi, i: (bi * nq + i, col('i_q', 1024))),
                  pl.BlockSpec((s, 128), lambda bi, i: (bi, col('a_k', 128))),
                  pl.BlockSpec((s, 128), lambda bi, i: (bi, col('a_v', 128))),
                  pl.BlockSpec((s, 128), lambda bi, i: (bi, col('i_kw', 128))),
                  pl.BlockSpec((1, HEAD_DIM), lambda bi, i: (0, 0)),
                  pl.BlockSpec((1, HEAD_DIM), lambda bi, i: (0, 0)),
                  pl.BlockSpec((A_HEADS, 3, QB, QB), lambda bi, i: (0, 0, 0, 0))],
        out_specs=pl.BlockSpec((QB, A_HEADS * HEAD_DIM), lambda bi, i: (bi * nq + i, 0)),
        out_shape=jax.ShapeDtypeStruct((b * s, A_HEADS * HEAD_DIM), BF16),
        scratch_shapes=[pltpu.VMEM((s, HEAD_DIM), BF16),
                        pltpu.VMEM((s // QB, HEAD_DIM, QB), BF16),
                        pltpu.VMEM((s, IDX_DIM), BF16),
                        pltpu.VMEM((s, QB), I32),
                        pltpu.VMEM((s, QB), F32)],
        compiler_params=_cparams("arbitrary", "arbitrary"),
        name="dsa")(z, z, z, z, z, qg, kg, bias_t)


def _dsa_bias_tiles(bias_a):
    assert MAX_DIST <= QB + 1
    ks = jnp.arange(QB)[:, None]
    qs = jnp.arange(QB)[None, :]
    tiles = [bias_a[_rel_bucket(qs - ks + back * QB)] for back in range(3)]
    return jnp.stack(tiles, axis=0).transpose(3, 0, 1, 2)


def _swa_kernel(q_ref, kc_ref, kp_ref, vc_ref, vp_ref, qg_ref, kg_ref, sink_ref, bias_ref, o_ref):
    n = pl.program_id(1)
    w = WINDOW
    g = B_HEADS // B_KV_HEADS
    scale = HEAD_DIM ** -0.5
    qi = lax.broadcasted_iota(I32, (w, 2 * w), 0)
    kj = lax.broadcasted_iota(I32, (w, 2 * w), 1)
    dist = qi + w - kj
    mask = (dist >= 0) & (dist < w) & ((kj >= w) | (n > 0))
    q = q_ref[...]
    for kv in range(B_KV_HEADS):
        sl = slice(kv * HEAD_DIM, (kv + 1) * HEAD_DIM)
        kw = jnp.concatenate([kp_ref[:, sl], kc_ref[:, sl]], axis=0)
        kn = _rms(kw, kg_ref[...]).astype(BF16)
        vw = jnp.concatenate([vp_ref[:, sl], vc_ref[:, sl]], axis=0).astype(BF16)
        for gi in range(g):
            h = kv * g + gi
            hs = slice(h * HEAD_DIM, (h + 1) * HEAD_DIM)
            qn = _rms(q[:, hs], qg_ref[...]).astype(BF16)
            lt = lax.dot_general(qn, kn, NT, preferred_element_type=F32) * scale
            lt = jnp.where(mask, lt + bias_ref[h], -jnp.inf)
            sink = sink_ref[h]
            m = jnp.maximum(jnp.max(lt, axis=-1, keepdims=True), sink)
            e = jnp.exp(lt - m)
            p = e / (jnp.sum(e, axis=-1, keepdims=True) + jnp.exp(sink - m))
            o_ref[:, hs] = jnp.dot(p.astype(BF16), vw, preferred_element_type=F32).astype(o_ref.dtype)


def _swa(z, off, qg, kg, sinks, bias_b, b, s):
    nb = s // WINDOW
    kcol, vcol = off['b_k'] // 256, off['b_v'] // 256
    cur = lambda c: (lambda bi, n: (bi * nb + n, c))
    prev = lambda c: (lambda bi, n: (bi * nb + jnp.maximum(n - 1, 0), c))
    return pl.pallas_call(
        _swa_kernel, grid=(b, nb),
        in_specs=[pl.BlockSpec((WINDOW, 1024), cur(off['b_q'] // 1024)),
                  pl.BlockSpec((WINDOW, 256), cur(kcol)),
                  pl.BlockSpec((WINDOW, 256), prev(kcol)),
                  pl.BlockSpec((WINDOW, 256), cur(vcol)),
                  pl.BlockSpec((WINDOW, 256), prev(vcol)),
                  pl.BlockSpec((1, HEAD_DIM), lambda bi, n: (0, 0)),
                  pl.BlockSpec((1, HEAD_DIM), lambda bi, n: (0, 0)),
                  pl.BlockSpec(memory_space=pltpu.SMEM),
                  pl.BlockSpec((B_HEADS, WINDOW, 2 * WINDOW), lambda bi, n: (0, 0, 0))],
        out_specs=pl.BlockSpec((WINDOW, B_HEADS * HEAD_DIM), lambda bi, n: (bi * nb + n, 0)),
        out_shape=jax.ShapeDtypeStruct((b * s, B_HEADS * HEAD_DIM), BF16),
        compiler_params=_cparams("arbitrary", "arbitrary"),
        name="swa")(z, z, z, z, z, qg, kg, sinks, bias_b)


def _swa_bias_tiles(bias_b):
    qi = jnp.arange(WINDOW)[:, None]
    kj = jnp.arange(2 * WINDOW)[None, :]
    return bias_b[_rel_bucket(qi + WINDOW - kj)].transpose(2, 0, 1)


def _gla_kernel(q_ref, k_ref, v_ref, r_ref, g_ref, gw_ref, gb_ref, ng_ref, o_ref, st_ref):
    s_len = q_ref.shape[0]
    st_ref[...] = jnp.zeros_like(st_ref)
    ri = lax.broadcasted_iota(I32, (CHUNK, CHUNK), 0)
    ci = lax.broadcasted_iota(I32, (CHUNK, CHUNK), 1)
    causal = ci <= ri
    tri = jnp.where(causal, 1.0, 0.0).astype(BF16)

    def body(n, carry):
        rows = pl.ds(pl.multiple_of(n * CHUNK, CHUNK), CHUNK)
        gl = jnp.dot(g_ref[rows, :].astype(BF16), gw_ref[...], preferred_element_type=F32) + gb_ref[...]
        la = (jnp.minimum(gl, 0.0) - jnp.log1p(jnp.exp(-jnp.abs(gl)))) / GATE_NORM
        hi = la.astype(BF16)
        lo = (la - hi.astype(F32)).astype(BF16)
        b = jnp.dot(tri, hi, preferred_element_type=F32) + jnp.dot(tri, lo, preferred_element_type=F32)
        bl = b[CHUNK - 1:CHUNK, :]
        q = q_ref[rows, :] * (C_DK ** -0.5)
        k = k_ref[rows, :]
        qg = (q * jnp.exp(b)).astype(BF16)
        kg = (k * jnp.exp(-b)).astype(BF16)
        kdec = (k * jnp.exp(bl - b)).astype(BF16)
        dec = jnp.exp(bl)
        vb = v_ref[rows, :].astype(BF16)
        rr = r_ref[rows, :]
        for h in range(C_HEADS):
            ks = slice(h * C_DK, (h + 1) * C_DK)
            vs = slice(h * C_DV, (h + 1) * C_DV)
            att = lax.dot_general(qg[:, ks], kg[:, ks], NT, preferred_element_type=F32)
            att = jnp.where(causal, att, 0.0).astype(BF16)
            st = st_ref[h]
            o = (jnp.dot(att, vb[:, vs], preferred_element_type=F32)
                 + lax.dot_general(qg[:, ks], st.astype(BF16), NT, preferred_element_type=F32))
            st_ref[h] = st * dec[:, ks] + lax.dot_general(vb[:, vs], kdec[:, ks], TN, preferred_element_type=F32)
            rh = rr[:, vs]
            o_ref[rows, vs] = (_rms(o, ng_ref[...]) * (rh * _sigmoid(rh))).astype(o_ref.dtype)
        return carry

    lax.fori_loop(0, s_len // CHUNK, body, 0)


def _gla(z, off, gw, gb, ng, b, s):
    blk = lambda name, w: pl.BlockSpec((s, w), lambda bi, c=off[name] // w: (bi, c))
    full = lambda shape: pl.BlockSpec(shape, lambda bi: (0,) * len(shape))
    return pl.pallas_call(
        _gla_kernel, grid=(b,),
        in_specs=[blk('c_q', 256), blk('c_k', 256), blk('c_v', 512), blk('c_r', 512), blk('c_g', 128),
                  full((LANES, C_HEADS * C_DK)), full((1, C_HEADS * C_DK)), full((1, C_DV))],
        out_specs=pl.BlockSpec((s, C_HEADS * C_DV), lambda bi: (bi, 0)),
        out_shape=jax.ShapeDtypeStruct((b * s, C_HEADS * C_DV), BF16),
        scratch_shapes=[pltpu.VMEM((C_HEADS, C_DV, C_DK), F32)],
        compiler_params=_cparams("arbitrary"),
        name="gla")(z, z, z, z, z, gw, gb, ng)


def _merge_kernel(oa_ref, ob_ref, oc_ref, ga_ref, gb_ref, gc_ref, h_ref, wa_ref, wb_ref, wc_ref, wo_ref,
                  n2_ref, wrh_ref, wrl_ref, br_ref, h1_ref, xn_ref, route_ref):
    dot = lambda a, w: jnp.dot(a, w, preferred_element_type=F32)
    merged = (_sigmoid(ga_ref[...]) * dot(oa_ref[...], wa_ref[...])
              + _sigmoid(gb_ref[...]) * dot(ob_ref[...], wb_ref[...])
              + _sigmoid(gc_ref[...]) * dot(oc_ref[...], wc_ref[...]))
    h1 = h_ref[...] + dot(merged.astype(BF16), wo_ref[...])
    h1_ref[...] = h1
    xn = _rms(h1, n2_ref[...])
    xn_ref[...] = xn

    xh = xn.astype(BF16)
    xl = (xn - xh.astype(F32)).astype(BF16)
    lg = dot(xh, wrh_ref[...]) + dot(xl, wrh_ref[...]) + dot(xh, wrl_ref[...]) + br_ref[...]
    lane = lax.broadcasted_iota(I32, lg.shape, 1)
    rmax = lambda x: jnp.max(x, axis=-1, keepdims=True)
    rsum = lambda x: jnp.sum(x, axis=-1, keepdims=True)
    first = lambda cond: jnp.min(jnp.where(cond, lane, LANES), axis=-1, keepdims=True)

    is_g = lane < N_GROUPS
    lgm = jnp.where(is_g, lg, -jnp.inf)
    mg = rmax(lgm)
    grp = first(lgm == mg)
    gw = 1.0 / rsum(jnp.where(is_g, jnp.exp(lgm - mg), 0.0))

    is_e = (lane >= N_GROUPS) & (lane < N_GROUPS + N_EXPERTS) & (((lane - N_GROUPS) >> 3) == grp)
    lem = jnp.where(is_e, lg, -jnp.inf)
    ee = jnp.where(is_e, jnp.exp(lem - rmax(lem)), 0.0)
    pe = jnp.where(is_e, ee / rsum(ee), -1.0)
    p1 = rmax(pe)
    i1 = first(pe == p1)
    pe2 = jnp.where(lane == i1, -1.0, pe)
    p2 = rmax(pe2)
    i2 = first(pe2 == p2)
    den = p1 + p2
    route_ref[...] = jnp.where(lane == 0, (i1 - N_GROUPS).astype(F32),
                               jnp.where(lane == 1, (i2 - N_GROUPS).astype(F32),
                                         jnp.where(lane == 2, gw * (p1 / den),
                                                   jnp.where(lane == 3, gw * (p2 / den), 0.0))))


def _merge(oa, ob, oc, z, h2d, off, wa, wb, wc, wo, n2, wrh, wrl, br):
    t, d = h2d.shape
    tm = 256
    row = lambda w, c=0: pl.BlockSpec((tm, w), lambda i: (i, c))
    const = lambda a: pl.BlockSpec(a.shape, lambda i: (0,) * a.ndim, pipeline_mode=pl.Buffered(1))
    return pl.pallas_call(
        _merge_kernel, grid=(t // tm,),
        in_specs=[row(oa.shape[1]), row(ob.shape[1]), row(oc.shape[1]),
                  row(d, off['g_a'] // d), row(d, off['g_b'] // d), row(d, off['g_c'] // d), row(d),
                  const(wa), const(wb), const(wc), const(wo), const(n2), const(wrh), const(wrl), const(br)],
        out_specs=[row(d), row(d), row(LANES)],
        out_shape=[jax.ShapeDtypeStruct((t, d), F32), jax.ShapeDtypeStruct((t, d), F32),
                   jax.ShapeDtypeStruct((t, LANES), F32)],
        compiler_params=_cparams("arbitrary"),
        name="merge")(oa, ob, oc, z, z, z, h2d, wa, wb, wc, wo, n2, wrh, wrl, br)


def _pos_kernel(route_ref, pos_ref, pend_ref, carry_ref, pstart_ref):
    p = pl.program_id(0)
    j = pl.program_id(1)
    tm = route_ref.shape[0]
    lane = lax.broadcasted_iota(I32, (tm, LANES), 1)

    @pl.when(j == 0)
    def _():
        carry_ref[...] = jnp.zeros_like(carry_ref)

    r = route_ref[...]
    oh0 = lane == r[:, 0:1].astype(I32)
    oh1 = lane == r[:, 1:2].astype(I32)
    oh = jnp.where(oh0 | oh1, 1.0, 0.0)
    colsum = jnp.sum(oh, axis=0, keepdims=True)

    @pl.when(p == 0)
    def _():
        carry_ref[...] = carry_ref[...] + colsum

        @pl.when(j == pl.num_programs(1) - 1)
        def _():
            nblk = jnp.floor((carry_ref[...] + (MOE_ROWS - 1)) * (1.0 / MOE_ROWS))
            nb8 = jnp.broadcast_to(nblk, (8, LANES))
            hi = jnp.floor(nb8 * (1.0 / 16.0))
            lo = nb8 - hi * 16.0
            ui = lax.broadcasted_iota(I32, (LANES, LANES), 0)
            uj = lax.broadcasted_iota(I32, (LANES, LANES), 1)
            upper = jnp.where(ui <= uj, 1.0, 0.0).astype(BF16)
            incl = (16.0 * jnp.dot(hi.astype(BF16), upper, preferred_element_type=F32)
                    + jnp.dot(lo.astype(BF16), upper, preferred_element_type=F32))[0:1, :]
            pstart_ref[...] = (incl - nblk) * MOE_ROWS
            pend_ref[...] = (incl * MOE_ROWS).astype(I32)

    @pl.when(p == 1)
    def _():
        ri = lax.broadcasted_iota(I32, (tm, tm), 0)
        ci = lax.broadcasted_iota(I32, (tm, tm), 1)
        strict = jnp.where(ci < ri, 1.0, 0.0).astype(BF16)
        base = jnp.dot(strict, oh.astype(BF16), preferred_element_type=F32) + carry_ref[...] + pstart_ref[...]
        d0 = jnp.sum(jnp.where(oh0, base, 0.0), axis=-1, keepdims=True)
        d1 = jnp.sum(jnp.where(oh1, base, 0.0), axis=-1, keepdims=True)
        pos_ref[...] = jnp.where(lane == 0, d0, jnp.where(lane == 1, d1, 0.0)).astype(I32)
        carry_ref[...] = carry_ref[...] + colsum


def _pos(route):
    t = route.shape[0]
    tm = 256
    return pl.pallas_call(
        _pos_kernel, grid=(2, t // tm),
        in_specs=[pl.BlockSpec((tm, LANES), lambda p, j: (j, 0))],
        out_specs=[pl.BlockSpec((tm, LANES), lambda p, j: (j * p, 0)),
                   pl.BlockSpec((1, LANES), lambda p, j: (0, 0))],
        out_shape=[jax.ShapeDtypeStruct((t, LANES), I32), jax.ShapeDtypeStruct((1, LANES), I32)],
        scratch_shapes=[pltpu.VMEM((1, LANES), F32), pltpu.VMEM((1, LANES), F32)],
        compiler_params=_cparams("arbitrary", "arbitrary"),
        name="moe_pos")(route)


def _dispatch_kernel(p0_ref, p1_ref, x_ref, xs_in_ref, xs_ref, sem):
    del xs_in_ref
    tm = x_ref.shape[0]
    base = pl.program_id(0) * tm

    def row_copy(r, dst):
        return pltpu.make_async_copy(x_ref.at[pl.ds(r, 1)], xs_ref.at[pl.ds(dst, 1)], sem)

    def issue(r, carry):
        row_copy(r, p0_ref[base + r]).start()
        row_copy(r, p1_ref[base + r]).start()
        return carry

    lax.fori_loop(0, tm, issue, 0)

    def drain(r, carry):
        row_copy(r, 0).wait()
        row_copy(r, 0).wait()
        return carry

    lax.fori_loop(0, tm, drain, 0)


def _dispatch(xn, pos0, pos1, n_rows):
    t, d = xn.shape
    tm = 256
    xs0 = jnp.zeros((n_rows, d), F32)
    return pl.pallas_call(
        _dispatch_kernel,
        grid_spec=pltpu.PrefetchScalarGridSpec(
            num_scalar_prefetch=2, grid=(t // tm,),
            in_specs=[pl.BlockSpec((tm, d), lambda i, p0, p1: (i, 0)),
                      pl.BlockSpec(memory_space=pl.ANY)],
            out_specs=pl.BlockSpec(memory_space=pl.ANY),
            scratch_shapes=[pltpu.SemaphoreType.DMA(())]),
        out_shape=jax.ShapeDtypeStruct((n_rows, d), F32),
        input_output_aliases={3: 0},
        compiler_params=_cparams("arbitrary"),
        name="moe_dispatch")(pos0, pos1, xn, xs0)


def _experts_kernel(be_ref, nu_ref, x_ref, w1_ref, w3_ref, w2_ref, o_ref):
    j = pl.program_id(0)

    @pl.when(j < nu_ref[0])
    def _():
        x = x_ref[...].astype(BF16)
        a = jnp.dot(x, w1_ref[0], preferred_element_type=F32)
        g = jnp.dot(x, w3_ref[0], preferred_element_type=F32)
        hdn = (a * _sigmoid(a) * g).astype(BF16)
        o_ref[...] = jnp.dot(hdn, w2_ref[0], preferred_element_type=F32)

    @pl.when(j >= nu_ref[0])
    def _():
        o_ref[...] = jnp.zeros_like(o_ref)


def _experts(xs, w1, w3, w2, blk_exp, n_used):
    p, d = xs.shape
    nblk = p // MOE_ROWS
    de = w1.shape[2]
    xmap = lambda j, be, nu: (jnp.minimum(j, nu[0] - 1), 0)
    wmap = lambda j, be, nu: (be[j], 0, 0)
    return pl.pallas_call(
        _experts_kernel,
        grid_spec=pltpu.PrefetchScalarGridSpec(
            num_scalar_prefetch=2, grid=(nblk,),
            in_specs=[pl.BlockSpec((MOE_ROWS, d), xmap),
                      pl.BlockSpec((1, d, de), wmap), pl.BlockSpec((1, d, de), wmap),
                      pl.BlockSpec((1, de, d), wmap)],
            out_specs=pl.BlockSpec((MOE_ROWS, d), lambda j, be, nu: (j, 0))),
        out_shape=jax.ShapeDtypeStruct((p, d), F32),
        compiler_params=_cparams("arbitrary"),
        name="moe_experts")(blk_exp, n_used, xs, w1, w3, w2)


def _combine_kernel(p0_ref, p1_ref, ys_ref, route_ref, h_ref, o_ref, b0_ref, b1_ref, sem):
    tm = h_ref.shape[0]
    base = pl.program_id(0) * tm

    def row_copy(src, buf, r):
        return pltpu.make_async_copy(ys_ref.at[pl.ds(src, 1)], buf.at[pl.ds(r, 1)], sem)

    def issue(r, carry):
        row_copy(p0_ref[base + r], b0_ref, r).start()
        row_copy(p1_ref[base + r], b1_ref, r).start()
        return carry

    lax.fori_loop(0, tm, issue, 0)

    def drain(r, carry):
        row_copy(0, b0_ref, r).wait()
        row_copy(0, b1_ref, r).wait()
        return carry

    lax.fori_loop(0, tm, drain, 0)
    rt = route_ref[...]
    o_ref[...] = h_ref[...] + (b0_ref[...] * rt[:, 2:3] + b1_ref[...] * rt[:, 3:4])


def _combine(ys, pos0, pos1, route, h1):
    t, d = h1.shape
    tm = 128
    return pl.pallas_call(
        _combine_kernel,
        grid_spec=pltpu.PrefetchScalarGridSpec(
            num_scalar_prefetch=2, grid=(t // tm,),
            in_specs=[pl.BlockSpec(memory_space=pl.ANY),
                      pl.BlockSpec((tm, LANES), lambda i, p0, p1: (i, 0)),
                      pl.BlockSpec((tm, d), lambda i, p0, p1: (i, 0))],
            out_specs=pl.BlockSpec((tm, d), lambda i, p0, p1: (i, 0)),
            scratch_shapes=[pltpu.VMEM((tm, d), F32), pltpu.VMEM((tm, d), F32), pltpu.SemaphoreType.DMA(())]),
        out_shape=jax.ShapeDtypeStruct((t, d), F32),
        compiler_params=_cparams("arbitrary"),
        name="moe_combine")(pos0, pos1, ys, route, h1)


def kernel(x, rel_bias, norm1_g, w_in, a_qn_g, a_kn_g, b_qn_g, b_kn_g, b_sinks, c_gate_w, c_gate_b, c_norm_g,
           w_br_a, w_br_b, w_br_c, w_o, norm2_g, w_rg, b_rg, w_re, b_re, w1, w3, w2):
    b, s, d = x.shape
    t = b * s
    depth = w_in.shape[0]
    assert s % QB == 0 and t % 512 == 0 and d % 1024 == 0
    n_keep = min(TOPK_MAX, s // 4)
    off, _ = _z_layout(d)
    bias_a_t = _dsa_bias_tiles(rel_bias[:, :A_HEADS])
    bias_b_t = _swa_bias_tiles(rel_bias[:, A_HEADS:])
    n_rows = t * 2 + N_EXPERTS * MOE_ROWS
    nblk = n_rows // MOE_ROWS
    row2 = lambda v: v.reshape(1, -1)

    h = x.reshape(t, d)
    for l in range(depth):
        z = _inproj(h, row2(norm1_g[l]), _pack_w_in(w_in[l], d))
        oa = _dsa(z, off, row2(a_qn_g[l]), row2(a_kn_g[l]), bias_a_t, b, s, n_keep)
        ob = _swa(z, off, row2(b_qn_g[l]), row2(b_kn_g[l]), b_sinks[l], bias_b_t, b, s)
        gw = jnp.zeros((LANES, C_HEADS * C_DK), F32).at[:GATE_RANK].set(c_gate_w[l]).astype(BF16)
        oc = _gla(z, off, gw, row2(c_gate_b[l]), row2(c_norm_g[l]), b, s)

        wr = jnp.zeros((d, LANES), F32).at[:, :N_GROUPS].set(w_rg[l]).at[:, N_GROUPS:N_GROUPS + N_EXPERTS].set(w_re[l])
        wrh = wr.astype(BF16)
        wrl = (wr - wrh.astype(F32)).astype(BF16)
        br = jnp.zeros((1, LANES), F32).at[0, :N_GROUPS].set(b_rg[l]).at[0, N_GROUPS:N_GROUPS + N_EXPERTS].set(b_re[l])
        h1, xn2, route = _merge(oa, ob, oc, z, h, off, w_br_a[l].astype(BF16), w_br_b[l].astype(BF16),
                                w_br_c[l].astype(BF16), w_o[l].astype(BF16), row2(norm2_g[l]), wrh, wrl, br)

        pos, pend = _pos(route)
        pos0, pos1 = pos[:, 0], pos[:, 1]
        pends = pend[0, :N_EXPERTS]
        blk_exp = jnp.minimum(jnp.searchsorted(pends, jnp.arange(nblk, dtype=I32) * MOE_ROWS, side='right'),
                              N_EXPERTS - 1).astype(I32)
        n_used = (pends[N_EXPERTS - 1:] // MOE_ROWS).astype(I32)
        xs = _dispatch(xn2, pos0, pos1, n_rows)
        ys = _experts(xs, w1[l].astype(BF16), w3[l].astype(BF16), w2[l].astype(BF16), blk_exp, n_used)
        h = _combine(ys, pos0, pos1, route, h1)
    return h.reshape(b, s, d)
```

```python
import functools
import math

import jax
import jax.numpy as jnp
from jax import lax
from jax.experimental import pallas as pl
from jax.experimental.pallas import tpu as pltpu

F32 = jnp.float32
BF16 = jnp.bfloat16
I32 = jnp.int32

HEAD_DIM = 128
A_HEADS = 4
IDX_HEADS = 16
IDX_DIM = 64
TOPK_MAX = 256
B_HEADS = 8
B_KV_HEADS = 2
WINDOW = 128
C_HEADS = 4
C_DK = 64
C_DV = 128
GATE_RANK = 16
GATE_NORM = 16.0
CHUNK = 64
N_BUCKETS = 32
MAX_DIST = 128
N_GROUPS = 8
EXP_PER_GROUP = 8
N_EXPERTS = N_GROUPS * EXP_PER_GROUP
D_EXPERT = 512
EPS = 1e-6

LANES = 128
QB = 256
MOE_ROWS = 256
INPROJ_TM, INPROJ_TN = 1024, 1024
MERGE_TM = 256
POS_TM = 256
DISPATCH_TM = 256
COMBINE_TM = 128
DMA_UNROLL = 8
VMEM_LIMIT = 52 * 1024 * 1024

INT_MIN = -(2 ** 31)
NT = (((1,), (1,)), ((), ()))
TN = (((0,), (0,)), ((), ()))


def _cparams(*sem):
    return pltpu.CompilerParams(dimension_semantics=sem, vmem_limit_bytes=VMEM_LIMIT)


def _rms(x, g):
    ms = jnp.mean(x * x, axis=-1, keepdims=True)
    return x * lax.rsqrt(ms + EPS) * g


def _sigmoid(x):
    return 1.0 / (1.0 + jnp.exp(-x))


def _z_layout(d):
    order = (('g_a', d), ('g_b', d), ('g_c', d), ('i_q', 1024), ('b_q', 1024), ('a_q', 512), ('c_v', 512),
             ('c_r', 512), ('b_k', 256), ('b_v', 256), ('c_q', 256), ('c_k', 256), ('a_k', 128), ('a_v', 128),
             ('i_kw', 128), ('c_g', 128))
    off, o = {}, 0
    for name, n in order:
        assert o % n == 0
        off[name] = o
        o += n
    return off, o


def _pack_w_in(w, d):
    splits = (('a_q', 512), ('a_k', 128), ('a_v', 128), ('i_q', 1024), ('i_k', 64), ('i_w', 16), ('b_q', 1024),
              ('b_k', 256), ('b_v', 256), ('c_q', 256), ('c_k', 256), ('c_v', 512), ('c_g', 16), ('c_r', 512),
              ('g_a', d), ('g_b', d), ('g_c', d))
    src, o = {}, 0
    for name, n in splits:
        src[name] = w[:, o:o + n]
        o += n
    zpad = lambda n: jnp.zeros((w.shape[0], n), w.dtype)
    src['i_kw'] = jnp.concatenate([src['i_k'], src['i_w'], zpad(LANES - IDX_DIM - IDX_HEADS)], axis=1)
    src['c_g'] = jnp.concatenate([src['c_g'], zpad(LANES - GATE_RANK)], axis=1)
    off, nz = _z_layout(d)
    names = sorted(off, key=off.get)
    return jnp.concatenate([src[n] for n in names], axis=1).astype(BF16)


def _rel_bucket(dist):
    max_exact = N_BUCKETS // 2
    d = jnp.maximum(dist, 0)
    large = max_exact + (jnp.log(jnp.maximum(d, 1).astype(F32) / max_exact)
                         / math.log(MAX_DIST / max_exact) * (N_BUCKETS - max_exact)).astype(I32)
    large = jnp.minimum(large, N_BUCKETS - 1)
    return jnp.where(d < max_exact, d, large)


def _inproj_kernel(x_ref, g_ref, w_ref, o_ref, xn_ref):
    @pl.when(pl.program_id(1) == 0)
    def _():
        xn_ref[...] = _rms(x_ref[...], g_ref[...]).astype(BF16)

    o_ref[...] = jnp.dot(xn_ref[...], w_ref[...], preferred_element_type=F32).astype(o_ref.dtype)


def _inproj(h2d, g, wz):
    t, d = h2d.shape
    nz = wz.shape[1]
    tm, tn = INPROJ_TM, INPROJ_TN
    return pl.pallas_call(
        _inproj_kernel, grid=(t // tm, nz // tn),
        in_specs=[pl.BlockSpec((tm, d), lambda i, j: (i, 0)),
                  pl.BlockSpec((1, d), lambda i, j: (0, 0)),
                  pl.BlockSpec((d, tn), lambda i, j: (0, j))],
        out_specs=pl.BlockSpec((tm, tn), lambda i, j: (i, j)),
        out_shape=jax.ShapeDtypeStruct((t, nz), BF16),
        scratch_shapes=[pltpu.VMEM((tm, d), BF16)],
        compiler_params=_cparams("arbitrary", "arbitrary"),
        name="inproj")(h2d, g, wz)


def _dsa_kernel(aq_ref, iq_ref, ak_ref, av_ref, ikw_ref, qg_ref, kg_ref, bias_ref, o_ref,
                kn_ref, vt_ref, ik_ref, key_ref, lg_ref, acc_ref, *, n_keep):
    i = pl.program_id(1)
    s_len = ak_ref.shape[0]
    nch = s_len // QB
    row0 = pl.multiple_of(i * QB, QB)
    chunk_rows = lambda c: pl.ds(pl.multiple_of(c * QB, QB), QB)

    @pl.when(i == 0)
    def _prep():
        kn_ref[...] = _rms(ak_ref[...].astype(F32), kg_ref[...]).astype(BF16)
        ik_ref[...] = ikw_ref[:, :IDX_DIM]
        for c in range(nch):
            vt_ref[c] = av_ref[c * QB:(c + 1) * QB, :].astype(F32).T.astype(BF16)

    key_pos = lax.broadcasted_iota(I32, (QB, QB), 0)
    qry_pos = lax.broadcasted_iota(I32, (QB, QB), 1) + row0

    iw_t = ikw_ref[pl.ds(row0, QB), :].astype(F32).T
    iq = iq_ref[...]
    iscale = (IDX_HEADS ** -0.5) * (IDX_DIM ** -0.5)

    def idx_body(c, carry):
        rows = chunk_rows(c)
        ikc = ik_ref[rows, :]
        acc = jnp.zeros((QB, QB), F32)
        for h in range(IDX_HEADS):
            rel = lax.dot_general(ikc, iq[:, h * IDX_DIM:(h + 1) * IDX_DIM], NT, preferred_element_type=F32)
            acc = acc + jnp.maximum(rel, 0.0) * iw_t[IDX_DIM + h:IDX_DIM + h + 1, :]
        score = jnp.where(key_pos + c * QB <= qry_pos, acc * iscale, -jnp.inf)
        bits = lax.bitcast_convert_type(score, I32)
        key_ref[rows, :] = jnp.where(bits < 0, bits ^ jnp.int32(0x7FFFFFFF), bits)
        return carry

    lax.fori_loop(0, i + 1, idx_body, 0)

    def count(pred):
        def body(c, cnt):
            hit = jnp.where(pred(key_ref[chunk_rows(c), :]), 1.0, 0.0)
            return cnt + jnp.sum(hit.reshape(QB // 8, 8, QB), axis=0)
        cnt = lax.fori_loop(0, i + 1, body, jnp.zeros((8, QB), F32))
        return jnp.sum(cnt, axis=0, keepdims=True)

    def bit_body(it, thr):
        cand = thr + lax.shift_left(jnp.int32(1), 31 - it)
        return jnp.where(count(lambda key: key >= cand) >= n_keep, cand, thr)

    thr = lax.fori_loop(0, 32, bit_body, jnp.full((1, QB), INT_MIN, I32))

    n_ge = count(lambda key: key >= thr)
    masked_key = jnp.int32(-0x00800000 ^ 0x7FFFFFFF)
    has_extra = jnp.where((n_ge > n_keep) & (thr != masked_key), 1.0, 0.0)

    @pl.when(jnp.max(has_extra) > 0.0)
    def _ties():
        need = n_keep - count(lambda key: key > thr)
        before = jnp.where(lax.broadcasted_iota(I32, (QB, QB), 1) < key_pos, 1.0, 0.0).astype(BF16)

        def body(c, run):
            rows = chunk_rows(c)
            key = key_ref[rows, :]
            tie = key == thr
            tie_f = jnp.where(tie, 1.0, 0.0)
            earlier = jnp.dot(before, tie_f.astype(BF16), preferred_element_type=F32) + run
            key_ref[rows, :] = jnp.where(tie & (earlier >= need), thr - 1, key)
            return run + jnp.sum(tie_f, axis=0, keepdims=True)

        lax.fori_loop(0, i + 1, body, jnp.zeros((1, QB), F32))

    scale = HEAD_DIM ** -0.5
    aq = aq_ref[...].astype(F32)
    qn = jnp.concatenate([_rms(aq[:, h * HEAD_DIM:(h + 1) * HEAD_DIM], qg_ref[...]) for h in range(A_HEADS)],
                         axis=0).astype(BF16)

    def logit_body(c, ms):
        rows = chunk_rows(c)
        keep = (key_ref[rows, :] >= thr) & (key_pos + c * QB <= qry_pos)
        lt_all = lax.dot_general(kn_ref[rows, :], qn, NT, preferred_element_type=F32) * scale
        back = jnp.minimum(i - c, 2)
        out = []
        for h in range(A_HEADS):
            lt = jnp.where(keep, lt_all[:, h * QB:(h + 1) * QB] + bias_ref[h, back], -jnp.inf)
            lg_ref[h, rows, :] = lt
            out.append(jnp.maximum(ms[h], jnp.max(lt, axis=0, keepdims=True)))
        return tuple(out)

    ms = lax.fori_loop(0, i + 1, logit_body, tuple(jnp.full((1, QB), -jnp.inf, F32) for _ in range(A_HEADS)))
    acc_ref[...] = jnp.zeros_like(acc_ref)

    def pv_body(c, ls):
        rows = chunk_rows(c)
        es, out = [], []
        for h in range(A_HEADS):
            e = jnp.exp(lg_ref[h, rows, :] - ms[h])
            out.append(ls[h] + jnp.sum(e, axis=0, keepdims=True))
            es.append(e.astype(BF16))
        acc_ref[...] += jnp.dot(vt_ref[c], jnp.concatenate(es, axis=1), preferred_element_type=F32)
        return tuple(out)

    ls = lax.fori_loop(0, i + 1, pv_body, tuple(jnp.zeros((1, QB), F32) for _ in range(A_HEADS)))
    for h in range(A_HEADS):
        o_ref[:, h * HEAD_DIM:(h + 1) * HEAD_DIM] = (acc_ref[:, h * QB:(h + 1) * QB] / ls[h]).T.astype(o_ref.dtype)


def _dsa(z, off, qg, kg, bias_t, b, s, n_keep):
    nq = s // QB
    col = lambda name, w: off[name] // w
    return pl.pallas_call(
        functools.partial(_dsa_kernel, n_keep=n_keep), grid=(b, nq),
        in_specs=[pl.BlockSpec((QB, 512), lambda bi, i: (bi * nq + i, col('a_q', 512))),
                  pl.BlockSpec((QB, 1024), lambda bi, i: (bi * nq + i, col('i_q', 1024))),
                  pl.BlockSpec((s, 128), lambda bi, i: (bi, col('a_k', 128))),
                  pl.BlockSpec((s, 128), lambda bi, i: (bi, col('a_v', 128))),
                  pl.BlockSpec((s, 128), lambda bi, i: (bi, col('i_kw', 128))),
                  pl.BlockSpec((1, HEAD_DIM), lambda bi, i: (0, 0)),
                  pl.BlockSpec((1, HEAD_DIM), lambda bi, i: (0, 0)),
                  pl.BlockSpec((A_HEADS, 3, QB, QB), lambda bi, i: (0, 0, 0, 0))],
        out_specs=pl.BlockSpec((QB, A_HEADS * HEAD_DIM), lambda bi, i: (bi * nq + i, 0)),
        out_shape=jax.ShapeDtypeStruct((b * s, A_HEADS * HEAD_DIM), BF16),
        scratch_shapes=[pltpu.VMEM((s, HEAD_DIM), BF16),
                        pltpu.VMEM((s // QB, HEAD_DIM, QB), BF16),
                        pltpu.VMEM((s, IDX_DIM), BF16),
                        pltpu.VMEM((s, QB), I32),
                        pltpu.VMEM((A_HEADS, s, QB), F32),
                        pltpu.VMEM((HEAD_DIM, A_HEADS * QB), F32)],
        compiler_params=_cparams("arbitrary", "arbitrary"),
        name="dsa")(z, z, z, z, z, qg, kg, bias_t)


def _dsa_bias_tiles(bias_a):
    assert MAX_DIST <= QB + 1
    ks = jnp.arange(QB)[:, None]
    qs = jnp.arange(QB)[None, :]
    tiles = [bias_a[_rel_bucket(qs - ks + back * QB)] for back in range(3)]
    return jnp.stack(tiles, axis=0).transpose(3, 0, 1, 2)


def _swa_kernel(q_ref, kc_ref, kp_ref, vc_ref, vp_ref, qg_ref, kg_ref, sink_ref, bias_ref, o_ref):
    n = pl.program_id(1)
    w = WINDOW
    g = B_HEADS // B_KV_HEADS
    scale = HEAD_DIM ** -0.5
    qi = lax.broadcasted_iota(I32, (w, 2 * w), 0)
    kj = lax.broadcasted_iota(I32, (w, 2 * w), 1)
    dist = qi + w - kj
    mask = (dist >= 0) & (dist < w) & ((kj >= w) | (n > 0))
    q = q_ref[...].astype(F32)
    for kv in range(B_KV_HEADS):
        sl = slice(kv * HEAD_DIM, (kv + 1) * HEAD_DIM)
        kw = jnp.concatenate([kp_ref[:, sl], kc_ref[:, sl]], axis=0).astype(F32)
        kn = _rms(kw, kg_ref[...]).astype(BF16)
        vw = jnp.concatenate([vp_ref[:, sl], vc_ref[:, sl]], axis=0)
        for gi in range(g):
            h = kv * g + gi
            hs = slice(h * HEAD_DIM, (h + 1) * HEAD_DIM)
            qn = _rms(q[:, hs], qg_ref[...]).astype(BF16)
            lt = lax.dot_general(qn, kn, NT, preferred_element_type=F32) * scale
            lt = jnp.where(mask, lt + bias_ref[h], -jnp.inf)
            sink = sink_ref[h]
            m = jnp.maximum(jnp.max(lt, axis=-1, keepdims=True), sink)
            e = jnp.exp(lt - m)
            p = e / (jnp.sum(e, axis=-1, keepdims=True) + jnp.exp(sink - m))
            o_ref[:, hs] = jnp.dot(p.astype(BF16), vw, preferred_element_type=F32).astype(o_ref.dtype)


def _swa(z, off, qg, kg, sinks, bias_b, b, s):
    nb = s // WINDOW
    kcol, vcol = off['b_k'] // 256, off['b_v'] // 256
    cur = lambda c: (lambda bi, n: (bi * nb + n, c))
    prev = lambda c: (lambda bi, n: (bi * nb + jnp.maximum(n - 1, 0), c))
    return pl.pallas_call(
        _swa_kernel, grid=(b, nb),
        in_specs=[pl.BlockSpec((WINDOW, 1024), cur(off['b_q'] // 1024)),
                  pl.BlockSpec((WINDOW, 256), cur(kcol)),
                  pl.BlockSpec((WINDOW, 256), prev(kcol)),
                  pl.BlockSpec((WINDOW, 256), cur(vcol)),
                  pl.BlockSpec((WINDOW, 256), prev(vcol)),
                  pl.BlockSpec((1, HEAD_DIM), lambda bi, n: (0, 0)),
                  pl.BlockSpec((1, HEAD_DIM), lambda bi, n: (0, 0)),
                  pl.BlockSpec(memory_space=pltpu.SMEM),
                  pl.BlockSpec((B_HEADS, WINDOW, 2 * WINDOW), lambda bi, n: (0, 0, 0))],
        out_specs=pl.BlockSpec((WINDOW, B_HEADS * HEAD_DIM), lambda bi, n: (bi * nb + n, 0)),
        out_shape=jax.ShapeDtypeStruct((b * s, B_HEADS * HEAD_DIM), BF16),
        compiler_params=_cparams("arbitrary", "arbitrary"),
        name="swa")(z, z, z, z, z, qg, kg, sinks, bias_b)


def _swa_bias_tiles(bias_b):
    qi = jnp.arange(WINDOW)[:, None]
    kj = jnp.arange(2 * WINDOW)[None, :]
    return bias_b[_rel_bucket(qi + WINDOW - kj)].transpose(2, 0, 1)


def _gla_kernel(q_ref, k_ref, v_ref, r_ref, g_ref, gw_ref, gb_ref, ng_ref, o_ref, st_ref):
    s_len = q_ref.shape[0]
    st_ref[...] = jnp.zeros_like(st_ref)
    ri = lax.broadcasted_iota(I32, (CHUNK, CHUNK), 0)
    ci = lax.broadcasted_iota(I32, (CHUNK, CHUNK), 1)
    causal = ci <= ri
    tri = jnp.where(causal, 1.0, 0.0).astype(BF16)

    def body(n, carry):
        rows = pl.ds(pl.multiple_of(n * CHUNK, CHUNK), CHUNK)
        gl = jnp.dot(g_ref[rows, :], gw_ref[...], preferred_element_type=F32) + gb_ref[...]
        la = (jnp.minimum(gl, 0.0) - jnp.log1p(jnp.exp(-jnp.abs(gl)))) / GATE_NORM
        hi = la.astype(BF16)
        lo = (la - hi.astype(F32)).astype(BF16)
        b = jnp.dot(tri, hi, preferred_element_type=F32) + jnp.dot(tri, lo, preferred_element_type=F32)
        bl = b[CHUNK - 1:CHUNK, :]
        q = q_ref[rows, :].astype(F32) * (C_DK ** -0.5)
        k = k_ref[rows, :].astype(F32)
        qg = (q * jnp.exp(b)).astype(BF16)
        kg = (k * jnp.exp(-b)).astype(BF16)
        kdec = (k * jnp.exp(bl - b)).astype(BF16)
        dec = jnp.exp(bl)
        vb = v_ref[rows, :]
        rr = r_ref[rows, :].astype(F32)
        for h in range(C_HEADS):
            ks = slice(h * C_DK, (h + 1) * C_DK)
            vs = slice(h * C_DV, (h + 1) * C_DV)
            att = lax.dot_general(qg[:, ks], kg[:, ks], NT, preferred_element_type=F32)
            att = jnp.where(causal, att, 0.0).astype(BF16)
            st = st_ref[h]
            o = (jnp.dot(att, vb[:, vs], preferred_element_type=F32)
                 + lax.dot_general(qg[:, ks], st.astype(BF16), NT, preferred_element_type=F32))
            st_ref[h] = st * dec[:, ks] + lax.dot_general(vb[:, vs], kdec[:, ks], TN, preferred_element_type=F32)
            rh = rr[:, vs]
            o_ref[rows, vs] = (_rms(o, ng_ref[...]) * (rh * _sigmoid(rh))).astype(o_ref.dtype)
        return carry

    lax.fori_loop(0, s_len // CHUNK, body, 0)


def _gla(z, off, gw, gb, ng, b, s):
    blk = lambda name, w: pl.BlockSpec((s, w), lambda bi, c=off[name] // w: (bi, c))
    full = lambda shape: pl.BlockSpec(shape, lambda bi: (0,) * len(shape))
    return pl.pallas_call(
        _gla_kernel, grid=(b,),
        in_specs=[blk('c_q', 256), blk('c_k', 256), blk('c_v', 512), blk('c_r', 512), blk('c_g', 128),
                  full((LANES, C_HEADS * C_DK)), full((1, C_HEADS * C_DK)), full((1, C_DV))],
        out_specs=pl.BlockSpec((s, C_HEADS * C_DV), lambda bi: (bi, 0)),
        out_shape=jax.ShapeDtypeStruct((b * s, C_HEADS * C_DV), BF16),
        scratch_shapes=[pltpu.VMEM((C_HEADS, C_DV, C_DK), F32)],
        compiler_params=_cparams("arbitrary"),
        name="gla")(z, z, z, z, z, gw, gb, ng)


def _merge_kernel(oa_ref, ob_ref, oc_ref, ga_ref, gb_ref, gc_ref, h_ref, wa_ref, wb_ref, wc_ref, wo_ref,
                  n2_ref, wrh_ref, wrl_ref, br_ref, h1_ref, xn_ref, route_ref):
    dot = lambda a, w: jnp.dot(a, w, preferred_element_type=F32)
    gate = lambda g_ref: _sigmoid(g_ref[...].astype(F32))
    merged = (gate(ga_ref) * dot(oa_ref[...], wa_ref[...])
              + gate(gb_ref) * dot(ob_ref[...], wb_ref[...])
              + gate(gc_ref) * dot(oc_ref[...], wc_ref[...]))
    h1 = h_ref[...] + dot(merged.astype(BF16), wo_ref[...])
    h1_ref[...] = h1
    xn = _rms(h1, n2_ref[...])
    xn_ref[...] = xn

    xh = xn.astype(BF16)
    xl = (xn - xh.astype(F32)).astype(BF16)
    lg = dot(xh, wrh_ref[...]) + dot(xl, wrh_ref[...]) + dot(xh, wrl_ref[...]) + br_ref[...]
    lane = lax.broadcasted_iota(I32, lg.shape, 1)
    rmax = lambda x: jnp.max(x, axis=-1, keepdims=True)
    rsum = lambda x: jnp.sum(x, axis=-1, keepdims=True)
    first = lambda cond: jnp.min(jnp.where(cond, lane, LANES), axis=-1, keepdims=True)

    is_g = lane < N_GROUPS
    lgm = jnp.where(is_g, lg, -jnp.inf)
    mg = rmax(lgm)
    grp = first(lgm == mg)
    gw = 1.0 / rsum(jnp.where(is_g, jnp.exp(lgm - mg), 0.0))

    is_e = (lane >= N_GROUPS) & (lane < N_GROUPS + N_EXPERTS) & (((lane - N_GROUPS) >> 3) == grp)
    lem = jnp.where(is_e, lg, -jnp.inf)
    ee = jnp.where(is_e, jnp.exp(lem - rmax(lem)), 0.0)
    pe = jnp.where(is_e, ee / rsum(ee), -1.0)
    p1 = rmax(pe)
    i1 = first(pe == p1)
    pe2 = jnp.where(lane == i1, -1.0, pe)
    p2 = rmax(pe2)
    i2 = first(pe2 == p2)
    den = p1 + p2
    route_ref[...] = jnp.where(lane == 0, (i1 - N_GROUPS).astype(F32),
                               jnp.where(lane == 1, (i2 - N_GROUPS).astype(F32),
                                         jnp.where(lane == 2, gw * (p1 / den),
                                                   jnp.where(lane == 3, gw * (p2 / den), 0.0))))


def _merge(oa, ob, oc, z, h2d, off, wa, wb, wc, wo, n2, wrh, wrl, br):
    t, d = h2d.shape
    tm = MERGE_TM
    row = lambda w, c=0: pl.BlockSpec((tm, w), lambda i: (i, c))
    const = lambda a: pl.BlockSpec(a.shape, lambda i: (0,) * a.ndim, pipeline_mode=pl.Buffered(1))
    return pl.pallas_call(
        _merge_kernel, grid=(t // tm,),
        in_specs=[row(oa.shape[1]), row(ob.shape[1]), row(oc.shape[1]),
                  row(d, off['g_a'] // d), row(d, off['g_b'] // d), row(d, off['g_c'] // d), row(d),
                  const(wa), const(wb), const(wc), const(wo), const(n2), const(wrh), const(wrl), const(br)],
        out_specs=[row(d), row(d), row(LANES)],
        out_shape=[jax.ShapeDtypeStruct((t, d), F32), jax.ShapeDtypeStruct((t, d), F32),
                   jax.ShapeDtypeStruct((t, LANES), F32)],
        compiler_params=_cparams("arbitrary"),
        name="merge")(oa, ob, oc, z, z, z, h2d, wa, wb, wc, wo, n2, wrh, wrl, br)


def _pos_kernel(route_ref, pos_ref, pend_ref, carry_ref, pstart_ref):
    p = pl.program_id(0)
    j = pl.program_id(1)
    tm = route_ref.shape[0]
    lane = lax.broadcasted_iota(I32, (tm, LANES), 1)

    @pl.when(j == 0)
    def _():
        carry_ref[...] = jnp.zeros_like(carry_ref)

    r = route_ref[...]
    oh0 = lane == r[:, 0:1].astype(I32)
    oh1 = lane == r[:, 1:2].astype(I32)
    oh = jnp.where(oh0 | oh1, 1.0, 0.0)
    colsum = jnp.sum(oh, axis=0, keepdims=True)

    @pl.when(p == 0)
    def _():
        carry_ref[...] = carry_ref[...] + colsum

        @pl.when(j == pl.num_programs(1) - 1)
        def _():
            nblk = jnp.floor((carry_ref[...] + (MOE_ROWS - 1)) * (1.0 / MOE_ROWS))
            nb8 = jnp.broadcast_to(nblk, (8, LANES))
            hi = jnp.floor(nb8 * (1.0 / 16.0))
            lo = nb8 - hi * 16.0
            ui = lax.broadcasted_iota(I32, (LANES, LANES), 0)
            uj = lax.broadcasted_iota(I32, (LANES, LANES), 1)
            upper = jnp.where(ui <= uj, 1.0, 0.0).astype(BF16)
            incl = (16.0 * jnp.dot(hi.astype(BF16), upper, preferred_element_type=F32)
                    + jnp.dot(lo.astype(BF16), upper, preferred_element_type=F32))[0:1, :]
            pstart_ref[...] = (incl - nblk) * MOE_ROWS
            pend_ref[...] = (incl * MOE_ROWS).astype(I32)

    @pl.when(p == 1)
    def _():
        ri = lax.broadcasted_iota(I32, (tm, tm), 0)
        ci = lax.broadcasted_iota(I32, (tm, tm), 1)
        strict = jnp.where(ci < ri, 1.0, 0.0).astype(BF16)
        base = jnp.dot(strict, oh.astype(BF16), preferred_element_type=F32) + carry_ref[...] + pstart_ref[...]
        d0 = jnp.sum(jnp.where(oh0, base, 0.0), axis=-1, keepdims=True)
        d1 = jnp.sum(jnp.where(oh1, base, 0.0), axis=-1, keepdims=True)
        pos_ref[...] = jnp.where(lane == 0, d0, jnp.where(lane == 1, d1, 0.0)).astype(I32)
        carry_ref[...] = carry_ref[...] + colsum


def _pos(route):
    t = route.shape[0]
    tm = POS_TM
    return pl.pallas_call(
        _pos_kernel, grid=(2, t // tm),
        in_specs=[pl.BlockSpec((tm, LANES), lambda p, j: (j, 0))],
        out_specs=[pl.BlockSpec((tm, LANES), lambda p, j: (j * p, 0)),
                   pl.BlockSpec((1, LANES), lambda p, j: (0, 0))],
        out_shape=[jax.ShapeDtypeStruct((t, LANES), I32), jax.ShapeDtypeStruct((1, LANES), I32)],
        scratch_shapes=[pltpu.VMEM((1, LANES), F32), pltpu.VMEM((1, LANES), F32)],
        compiler_params=_cparams("arbitrary", "arbitrary"),
        name="moe_pos")(route)


def _dispatch_kernel(p0_ref, p1_ref, x_ref, xs_in_ref, xs_ref, sem):
    del xs_in_ref
    tm = x_ref.shape[0]
    base = pl.program_id(0) * tm

    def row_copy(r, dst):
        return pltpu.make_async_copy(x_ref.at[pl.ds(r, 1)], xs_ref.at[pl.ds(dst, 1)], sem)

    def issue(r, carry):
        row_copy(r, p0_ref[base + r]).start()
        row_copy(r, p1_ref[base + r]).start()
        return carry

    lax.fori_loop(0, tm, issue, 0, unroll=DMA_UNROLL)

    def drain(r, carry):
        row_copy(r, 0).wait()
        row_copy(r, 0).wait()
        return carry

    lax.fori_loop(0, tm, drain, 0, unroll=DMA_UNROLL)


def _dispatch(xn, pos0, pos1, n_rows):
    t, d = xn.shape
    tm = DISPATCH_TM
    xs0 = jnp.zeros((n_rows, d), F32)
    return pl.pallas_call(
        _dispatch_kernel,
        grid_spec=pltpu.PrefetchScalarGridSpec(
            num_scalar_prefetch=2, grid=(t // tm,),
            in_specs=[pl.BlockSpec((tm, d), lambda i, p0, p1: (i, 0)),
                      pl.BlockSpec(memory_space=pl.ANY)],
            out_specs=pl.BlockSpec(memory_space=pl.ANY),
            scratch_shapes=[pltpu.SemaphoreType.DMA(())]),
        out_shape=jax.ShapeDtypeStruct((n_rows, d), F32),
        input_output_aliases={3: 0},
        compiler_params=_cparams("arbitrary"),
        name="moe_dispatch")(pos0, pos1, xn, xs0)


def _experts_kernel(be_ref, nu_ref, x_ref, w1_ref, w3_ref, w2_ref, o_ref, w1b_ref, w3b_ref, w2b_ref):
    j = pl.program_id(0)
    used = j < nu_ref[0]
    new_expert = (j == 0) | (be_ref[j] != be_ref[jnp.maximum(j - 1, 0)])

    @pl.when(used & new_expert)
    def _():
        w1b_ref[...] = w1_ref[0, 0].astype(BF16)
        w3b_ref[...] = w3_ref[0, 0].astype(BF16)
        w2b_ref[...] = w2_ref[0, 0].astype(BF16)

    @pl.when(used)
    def _():
        x = x_ref[...].astype(BF16)
        a = jnp.dot(x, w1b_ref[...], preferred_element_type=F32)
        g = jnp.dot(x, w3b_ref[...], preferred_element_type=F32)
        hdn = (a * _sigmoid(a) * g).astype(BF16)
        o_ref[...] = jnp.dot(hdn, w2b_ref[...], preferred_element_type=F32)

    @pl.when(jnp.logical_not(used))
    def _():
        o_ref[...] = jnp.zeros_like(o_ref)


def _experts(xs, w1, w3, w2, layer, blk_exp, n_used):
    p, d = xs.shape
    nblk = p // MOE_ROWS
    de = w1.shape[3]
    xmap = lambda j, be, nu: (jnp.minimum(j, nu[0] - 1), 0)
    wmap = lambda j, be, nu: (layer, be[j], 0, 0)
    return pl.pallas_call(
        _experts_kernel,
        grid_spec=pltpu.PrefetchScalarGridSpec(
            num_scalar_prefetch=2, grid=(nblk,),
            in_specs=[pl.BlockSpec((MOE_ROWS, d), xmap),
                      pl.BlockSpec((1, 1, d, de), wmap), pl.BlockSpec((1, 1, d, de), wmap),
                      pl.BlockSpec((1, 1, de, d), wmap)],
            out_specs=pl.BlockSpec((MOE_ROWS, d), lambda j, be, nu: (j, 0)),
            scratch_shapes=[pltpu.VMEM((d, de), BF16), pltpu.VMEM((d, de), BF16), pltpu.VMEM((de, d), BF16)]),
        out_shape=jax.ShapeDtypeStruct((p, d), F32),
        compiler_params=_cparams("arbitrary"),
        name="moe_experts")(blk_exp, n_used, xs, w1, w3, w2)


def _combine_kernel(p0_ref, p1_ref, ys_ref, route_ref, h_ref, o_ref, b0_ref, b1_ref, sem):
    i = pl.program_id(0)
    tm = h_ref.shape[0]
    slot = i % 2

    def row_copy(src, buf, sl, r):
        return pltpu.make_async_copy(ys_ref.at[pl.ds(src, 1)], buf.at[sl, pl.ds(r, 1)], sem.at[sl])

    def issue_block(blk, sl):
        base = blk * tm

        def issue(r, carry):
            row_copy(p0_ref[base + r], b0_ref, sl, r).start()
            row_copy(p1_ref[base + r], b1_ref, sl, r).start()
            return carry

        lax.fori_loop(0, tm, issue, 0, unroll=DMA_UNROLL)

    @pl.when(i == 0)
    def _():
        issue_block(0, 0)

    @pl.when(i + 1 < pl.num_programs(0))
    def _():
        issue_block(i + 1, 1 - slot)

    def drain(r, carry):
        row_copy(0, b0_ref, slot, r).wait()
        row_copy(0, b1_ref, slot, r).wait()
        return carry

    lax.fori_loop(0, tm, drain, 0, unroll=DMA_UNROLL)
    rt = route_ref[...]
    o_ref[...] = h_ref[...] + (b0_ref[slot] * rt[:, 2:3] + b1_ref[slot] * rt[:, 3:4])


def _combine(ys, pos0, pos1, route, h1):
    t, d = h1.shape
    tm = COMBINE_TM
    return pl.pallas_call(
        _combine_kernel,
        grid_spec=pltpu.PrefetchScalarGridSpec(
            num_scalar_prefetch=2, grid=(t // tm,),
            in_specs=[pl.BlockSpec(memory_space=pl.ANY),
                      pl.BlockSpec((tm, LANES), lambda i, p0, p1: (i, 0)),
                      pl.BlockSpec((tm, d), lambda i, p0, p1: (i, 0))],
            out_specs=pl.BlockSpec((tm, d), lambda i, p0, p1: (i, 0)),
            scratch_shapes=[pltpu.VMEM((2, tm, d), F32), pltpu.VMEM((2, tm, d), F32),
                            pltpu.SemaphoreType.DMA((2,))]),
        out_shape=jax.ShapeDtypeStruct((t, d), F32),
        compiler_params=_cparams("arbitrary"),
        name="moe_combine")(pos0, pos1, ys, route, h1)


def kernel(x, rel_bias, norm1_g, w_in, a_qn_g, a_kn_g, b_qn_g, b_kn_g, b_sinks, c_gate_w, c_gate_b, c_norm_g,
           w_br_a, w_br_b, w_br_c, w_o, norm2_g, w_rg, b_rg, w_re, b_re, w1, w3, w2):
    b, s, d = x.shape
    t = b * s
    depth = w_in.shape[0]
    assert s % QB == 0 and t % 512 == 0 and d % 1024 == 0
    n_keep = min(TOPK_MAX, s // 4)
    off, _ = _z_layout(d)
    bias_a_t = _dsa_bias_tiles(rel_bias[:, :A_HEADS])
    bias_b_t = _swa_bias_tiles(rel_bias[:, A_HEADS:])
    n_rows = t * 2 + N_EXPERTS * MOE_ROWS
    nblk = n_rows // MOE_ROWS
    row2 = lambda v: v.reshape(1, -1)

    h = x.reshape(t, d)
    for l in range(depth):
        z = _inproj(h, row2(norm1_g[l]), _pack_w_in(w_in[l], d))
        oa = _dsa(z, off, row2(a_qn_g[l]), row2(a_kn_g[l]), bias_a_t, b, s, n_keep)
        ob = _swa(z, off, row2(b_qn_g[l]), row2(b_kn_g[l]), b_sinks[l], bias_b_t, b, s)
        gw = jnp.zeros((LANES, C_HEADS * C_DK), F32).at[:GATE_RANK].set(c_gate_w[l]).astype(BF16)
        oc = _gla(z, off, gw, row2(c_gate_b[l]), row2(c_norm_g[l]), b, s)

        wr = jnp.zeros((d, LANES), F32).at[:, :N_GROUPS].set(w_rg[l]).at[:, N_GROUPS:N_GROUPS + N_EXPERTS].set(w_re[l])
        wrh = wr.astype(BF16)
        wrl = (wr - wrh.astype(F32)).astype(BF16)
        br = jnp.zeros((1, LANES), F32).at[0, :N_GROUPS].set(b_rg[l]).at[0, N_GROUPS:N_GROUPS + N_EXPERTS].set(b_re[l])
        h1, xn2, route = _merge(oa, ob, oc, z, h, off, w_br_a[l].astype(BF16), w_br_b[l].astype(BF16),
                                w_br_c[l].astype(BF16), w_o[l].astype(BF16), row2(norm2_g[l]), wrh, wrl, br)

        pos, pend = _pos(route)
        pos0, pos1 = pos[:, 0], pos[:, 1]
        pends = pend[0, :N_EXPERTS]
        blk_start = jnp.arange(nblk, dtype=I32) * MOE_ROWS
        blk_exp = jnp.minimum(jnp.sum((pends[None, :] <= blk_start[:, None]).astype(I32), axis=1), N_EXPERTS - 1)
        n_used = (pends[N_EXPERTS - 1:] // MOE_ROWS).astype(I32)
        xs = _dispatch(xn2, pos0, pos1, n_rows)
        ys = _experts(xs, w1, w3, w2, l, blk_exp, n_used)
        h = _combine(ys, pos0, pos1, route, h1)
    return h.reshape(b, s, d)
```
